```python
import math
import jax, jax.numpy as jnp
from jax import lax
import numpy as np


D_MODEL = 2048
BATCH = 8
SEQ = 2048
DEPTH = 1
DEC_BATCH = 32
DEC_SEQ = 8
PAST_LEN = 8192
PAGE_SIZE = 128

ATT_HEADS = 8
HEAD_DIM = 128
ATT_W = ATT_HEADS * HEAD_DIM
SSM_W = D_MODEL - ATT_W
SSM_GROUP = 16
SSM_GROUPS = SSM_W // SSM_GROUP
SSM_STATE = 64
D_FF = 5632
CONV_W = 3
Q_BLOCK = 128
N_MOD = 6
IN_COLS = 3 * ATT_W + SSM_W + 2 * D_MODEL
DN_ALPHA = (2.0 * DEPTH) ** 0.25
DN_BETA = (8.0 * DEPTH) ** -0.25
LN_EPS = 1e-5
SB_BIAS_LO = -7.0
SB_BIAS_HI = -5.0

kernel_name = 'stick_breaking_s5_hybrid_step'


def _layernorm(x, g, b):
    xf = x.astype(jnp.float32)
    mu = jnp.mean(xf, axis=-1, keepdims=True)
    var = jnp.mean(jnp.square(xf - mu), axis=-1, keepdims=True)
    y = (xf - mu) * lax.rsqrt(var + LN_EPS) * g.astype(jnp.float32) + b.astype(jnp.float32)
    return y.astype(x.dtype)


def _stick_breaking(q, k, v, bias, n_ctx):
    tq = q.shape[1]
    scale = HEAD_DIM ** -0.5
    b_h = bias.astype(jnp.float32)[None, :, None, None]
    outs = []
    for start in range(0, tq, Q_BLOCK):
        end = min(start + Q_BLOCK, tq)
        nk = n_ctx + end
        z = jnp.einsum('bqhd,bkhd->bhqk', q[:, start:end], k[:, :nk]).astype(jnp.float32) * scale + b_h
        q_pos = n_ctx + jnp.arange(start, end)
        k_pos = jnp.arange(nk)
        causal = k_pos[None, :] < q_pos[:, None]
        log_1m = jnp.where(causal, jax.nn.log_sigmoid(-z), 0.0)
        later = lax.cumsum(log_1m, axis=3, reverse=True) - log_1m
        w = jnp.where(causal, jnp.exp(jax.nn.log_sigmoid(z) + later), 0.0)
        outs.append(jnp.einsum('bhqk,bkhd->bqhd', w.astype(v.dtype), v[:, :nk]))
    return jnp.concatenate(outs, axis=1)


def _s5(u, h0, a_re, a_im, log_dt, b_re, b_im, c_re, c_im, d_skip):
    bsz, t, _ = u.shape
    uf = u.astype(jnp.float32).reshape(bsz, t, SSM_GROUPS, SSM_GROUP)
    lam = lax.complex(a_re.astype(jnp.float32), a_im.astype(jnp.float32))
    dt = jnp.exp(log_dt.astype(jnp.float32))[:, None]
    lam_bar = jnp.exp(lam * dt)
    b_c = lax.complex(b_re.astype(jnp.float32), b_im.astype(jnp.float32))
    b_bar = ((lam_bar - 1.0) / lam)[..., None] * b_c
    c_c = lax.complex(c_re.astype(jnp.float32), c_im.astype(jnp.float32))
    bu = jnp.einsum('btgn,gpn->btgp', uf, b_bar)
    a = jnp.broadcast_to(lam_bar, bu.shape)

    def combine(e1, e2):
        a1, x1 = e1
        a2, x2 = e2
        return a1 * a2, a2 * x1 + x2

    a_cum, h = lax.associative_scan(combine, (a, bu), axis=1)
    h = h + a_cum * h0[:, None]
    y = jnp.einsum('btgp,gnp->btgn', h, c_c).real
    y = y + d_skip.astype(jnp.float32).reshape(SSM_GROUPS, SSM_GROUP) * uf
    return y.reshape(bsz, t, SSM_W).astype(u.dtype), h[:, -1]


def _block(x, c, past_k, past_v, ssm_h0, conv_buf, lp):
    bsz, t, _ = x.shape
    mod = jax.nn.silu(c) @ lp['w_ada'] + lp['b_ada']
    sh1, sc1, g1, sh2, sc2, g2 = jnp.split(mod[:, None, :], N_MOD, axis=-1)

    h = x * (1.0 + sc1) + sh1
    cuts = [ATT_W, 2 * ATT_W, 3 * ATT_W, 3 * ATT_W + SSM_W, 3 * ATT_W + SSM_W + D_MODEL]
    q, k, v, u, ga, gb = jnp.split(h @ lp['w_in'], cuts, axis=-1)
    q = q.reshape(bsz, t, ATT_HEADS, HEAD_DIM)
    k = k.reshape(bsz, t, ATT_HEADS, HEAD_DIM)
    v = v.reshape(bsz, t, ATT_HEADS, HEAD_DIM)
    if past_k is None:
        n_ctx, k_all, v_all = 0, k, v
    else:
        n_ctx = past_k.shape[1]
        k_all = jnp.concatenate([past_k.astype(k.dtype), k], axis=1)
        v_all = jnp.concatenate([past_v.astype(v.dtype), v], axis=1)
    att = _stick_breaking(q, k_all, v_all, lp['sb_bias'], n_ctx).reshape(bsz, t, ATT_W)

    y_s, h_last = _s5(u, ssm_h0, lp['ssm_a_re'], lp['ssm_a_im'], lp['ssm_log_dt'],
                      lp['ssm_b_re'], lp['ssm_b_im'], lp['ssm_c_re'], lp['ssm_c_im'], lp['ssm_d'])
    y_s = jax.nn.gelu(y_s)
    y_s = y_s * jax.nn.sigmoid(y_s @ lp['w_glu'] + lp['b_glu'])

    merged = jax.nn.sigmoid(ga) * (att @ lp['w_att_br']) + jax.nn.sigmoid(gb) * (y_s @ lp['w_ssm_br'])
    x = _layernorm(DN_ALPHA * x + (1.0 + g1) * (merged @ lp['w_out']), lp['ln1_g'], lp['ln1_b'])

    h2 = x * (1.0 + sc2) + sh2
    up = h2 @ lp['w_up']
    ext = jnp.concatenate([conv_buf.astype(up.dtype), up], axis=1)
    conv = lp['conv_b']
    for i in range(CONV_W):
        conv = conv + lp['conv_w'][i] * ext[:, i:i + t]
    val, gate = jnp.split(conv, 2, axis=-1)
    f = (jax.nn.gelu(gate) * val) @ lp['w_down']
    x = _layernorm(DN_ALPHA * x + (1.0 + g2) * f, lp['ln2_g'], lp['ln2_b'])
    return x, k, v, h_last.real, h_last.imag, ext[:, t:]


def setup_inputs(seed: int = 0) -> dict:
    key = jax.random.key(seed)
    ks = jax.random.split(key, 40)
    f32 = jnp.float32
    n_pages = PAST_LEN // PAGE_SIZE
    n_phys = (DEC_BATCH * n_pages * 5) // 4
    L, D, F = DEPTH, D_MODEL, D_FF

    def nrm(k, shape, s):
        return jax.random.normal(k, shape, f32) * s

    page_table = jax.random.permutation(ks[7], n_phys)[:DEC_BATCH * n_pages]
    page_table = page_table.reshape(DEC_BATCH, n_pages).astype(jnp.int32)
    a_im0 = jnp.pi * jnp.arange(SSM_STATE, dtype=f32)
    return {
        'x_prompt': nrm(ks[0], (BATCH, SEQ, D), 1.0),
        'x_sample': nrm(ks[1], (DEC_BATCH, DEC_SEQ, D), 1.0),
        'cache_k': nrm(ks[2], (L, n_phys, PAGE_SIZE, ATT_HEADS, HEAD_DIM), 1.0),
        'cache_v': nrm(ks[3], (L, n_phys, PAGE_SIZE, ATT_HEADS, HEAD_DIM), 1.0),
        'state_ssm_re': nrm(ks[4], (L, DEC_BATCH, SSM_GROUPS, SSM_STATE), 0.1),
        'state_ssm_im': nrm(ks[5], (L, DEC_BATCH, SSM_GROUPS, SSM_STATE), 0.1),
        'state_conv': nrm(ks[6], (L, DEC_BATCH, CONV_W - 1, 2 * F), 1.0),
        'page_table': page_table,
        'c_prompt': nrm(ks[8], (BATCH, D), 1.0),
        'c_sample': nrm(ks[9], (DEC_BATCH, D), 1.0),
        'w_ada': nrm(ks[10], (L, D, N_MOD * D), 0.2 * D ** -0.5),
        'b_ada': nrm(ks[11], (L, N_MOD * D), 0.02),
        'w_in': nrm(ks[12], (L, D, IN_COLS), D ** -0.5),
        'w_att_br': nrm(ks[13], (L, ATT_W, D), ATT_W ** -0.5),
        'w_ssm_br': nrm(ks[14], (L, SSM_W, D), SSM_W ** -0.5),
        'w_out': nrm(ks[15], (L, D, D), DN_BETA * D ** -0.5),
        'sb_bias': jax.random.uniform(ks[34], (L, ATT_HEADS), f32, SB_BIAS_LO, SB_BIAS_HI),
        'ssm_a_re': -0.5 + nrm(ks[16], (L, SSM_GROUPS, SSM_STATE), 0.01),
        'ssm_a_im': a_im0 + nrm(ks[17], (L, SSM_GROUPS, SSM_STATE), 0.01),
        'ssm_log_dt': jax.random.uniform(ks[18], (L, SSM_GROUPS), f32, math.log(1e-3), math.log(1e-1)),
        'ssm_b_re': nrm(ks[19], (L, SSM_GROUPS, SSM_STATE, SSM_GROUP), (2.0 * SSM_GROUP) ** -0.5),
        'ssm_b_im': nrm(ks[20], (L, SSM_GROUPS, SSM_STATE, SSM_GROUP), (2.0 * SSM_GROUP) ** -0.5),
        'ssm_c_re': nrm(ks[21], (L, SSM_GROUPS, SSM_GROUP, SSM_STATE), SSM_STATE ** -0.5),
        'ssm_c_im': nrm(ks[22], (L, SSM_GROUPS, SSM_GROUP, SSM_STATE), SSM_STATE ** -0.5),
        'ssm_d': nrm(ks[23], (L, SSM_W), 1.0),
        'w_glu': nrm(ks[24], (L, SSM_W, SSM_W), SSM_W ** -0.5),
        'b_glu': nrm(ks[25], (L, SSM_W), 0.02),
        'ln1_g': 1.0 + nrm(ks[26], (L, D), 0.02),
        'ln1_b': nrm(ks[27], (L, D), 0.02),
        'w_up': nrm(ks[28], (L, D, 2 * F), D ** -0.5),
        'conv_w': nrm(ks[29], (L, CONV_W, 2 * F), CONV_W ** -0.5),
        'conv_b': nrm(ks[30], (L, 2 * F), 0.02),
        'w_down': nrm(ks[31], (L, F, D), DN_BETA * F ** -0.5),
        'ln2_g': 1.0 + nrm(ks[32], (L, D), 0.02),
        'ln2_b': nrm(ks[33], (L, D), 0.02),
    }


def reference(x_prompt, x_sample, cache_k, cache_v, state_ssm_re, state_ssm_im, state_conv, page_table,
              c_prompt, c_sample, w_ada, b_ada, w_in, w_att_br, w_ssm_br, w_out, sb_bias,
              ssm_a_re, ssm_a_im, ssm_log_dt, ssm_b_re, ssm_b_im, ssm_c_re, ssm_c_im, ssm_d,
              w_glu, b_glu, ln1_g, ln1_b, w_up, conv_w, conv_b, w_down, ln2_g, ln2_b):
    dec_b, n_pages = page_table.shape
    past_len = n_pages * cache_k.shape[2]
    xp, xs = x_prompt, x_sample
    kp_l, vp_l, srp_l, sip_l, cvp_l = [], [], [], [], []
    ks_l, vs_l, srs_l, sis_l, cvs_l = [], [], [], [], []
    for l in range(DEPTH):
        lp = dict(w_ada=w_ada[l], b_ada=b_ada[l], w_in=w_in[l], w_att_br=w_att_br[l],
                  w_ssm_br=w_ssm_br[l], w_out=w_out[l], sb_bias=sb_bias[l],
                  ssm_a_re=ssm_a_re[l], ssm_a_im=ssm_a_im[l],
                  ssm_log_dt=ssm_log_dt[l], ssm_b_re=ssm_b_re[l], ssm_b_im=ssm_b_im[l],
                  ssm_c_re=ssm_c_re[l], ssm_c_im=ssm_c_im[l], ssm_d=ssm_d[l], w_glu=w_glu[l],
                  b_glu=b_glu[l], ln1_g=ln1_g[l], ln1_b=ln1_b[l], w_up=w_up[l], conv_w=conv_w[l],
                  conv_b=conv_b[l], w_down=w_down[l], ln2_g=ln2_g[l], ln2_b=ln2_b[l])
        bp = xp.shape[0]
        h0_p = jnp.zeros((bp, SSM_GROUPS, SSM_STATE), jnp.complex64)
        buf_p = jnp.zeros((bp, CONV_W - 1, 2 * D_FF), xp.dtype)
        xp, kp, vp, srp, sip, cvp = _block(xp, c_prompt, None, None, h0_p, buf_p, lp)
        past_k = cache_k[l][page_table].reshape(dec_b, past_len, ATT_HEADS, HEAD_DIM)
        past_v = cache_v[l][page_table].reshape(dec_b, past_len, ATT_HEADS, HEAD_DIM)
        h0_s = lax.complex(state_ssm_re[l].astype(jnp.float32), state_ssm_im[l].astype(jnp.float32))
        xs, kss, vss, srs, sis, cvs = _block(xs, c_sample, past_k, past_v, h0_s, state_conv[l], lp)
        kp_l.append(kp); vp_l.append(vp); srp_l.append(srp.astype(xp.dtype))
        sip_l.append(sip.astype(xp.dtype)); cvp_l.append(cvp)
        ks_l.append(kss); vs_l.append(vss); srs_l.append(srs.astype(state_ssm_re.dtype))
        sis_l.append(sis.astype(state_ssm_im.dtype)); cvs_l.append(cvs.astype(state_conv.dtype))
    return (xp, xs, jnp.stack(kp_l), jnp.stack(vp_l), jnp.stack(srp_l), jnp.stack(sip_l), jnp.stack(cvp_l),
            jnp.stack(ks_l), jnp.stack(vs_l), jnp.stack(srs_l), jnp.stack(sis_l), jnp.stack(cvs_l))
```

```python
import functools
import math

import jax
import jax.numpy as jnp
from jax import lax
from jax.experimental import pallas as pl
from jax.experimental.pallas import tpu as pltpu

F32 = jnp.float32
BF16 = jnp.bfloat16
LN_EPS = 1e-5
N_MOD = 6
V7X_VMEM_LIMIT_BYTES = 56 * 1024 * 1024
LANES = 128
SUBLANES = 8


def _cparams(sem):
    return pltpu.CompilerParams(dimension_semantics=sem, vmem_limit_bytes=V7X_VMEM_LIMIT_BYTES)


def _dot(a, b):
    return jnp.dot(a, b, preferred_element_type=F32)


def _softplus(z):
    return jnp.maximum(z, 0.0) + jnp.log(1.0 + jnp.exp(-jnp.abs(z)))


def _gelu_tanh(x):
    c = math.sqrt(2.0 / math.pi)
    return 0.5 * x * (1.0 + jnp.tanh(c * (x + 0.044715 * (x * x * x))))


def _layernorm(x, g, b):
    mu = jnp.mean(x, axis=-1, keepdims=True)
    xc = x - mu
    var = jnp.mean(xc * xc, axis=-1, keepdims=True)
    return xc * lax.rsqrt(var + LN_EPS) * g + b


def _mod_kernel(c_ref, w_ref, b_ref, o_ref):
    c = c_ref[...]
    s = c * jax.nn.sigmoid(c)
    o_ref[...] = _dot(s.astype(BF16), w_ref[...].astype(BF16)) + b_ref[...]


def _mod_table(c, w_ada, b_ada):
    m, d = c.shape
    n = w_ada.shape[1]
    tn = min(1024, n)
    return pl.pallas_call(
        _mod_kernel,
        grid=(n // tn,),
        in_specs=[pl.BlockSpec((m, d), lambda j: (0, 0)),
                  pl.BlockSpec((d, tn), lambda j: (0, j)),
                  pl.BlockSpec((1, tn), lambda j: (0, j))],
        out_specs=pl.BlockSpec((m, tn), lambda j: (0, j)),
        out_shape=jax.ShapeDtypeStruct((m, n), F32),
        compiler_params=_cparams(("arbitrary",)),
        name="mod_table",
    )(c, w_ada, b_ada.reshape(1, n))


def _ssm_disc_kernel(ar_ref, ai_ref, ldt_ref, br_ref, bi_ref, lr_ref, li_ref, bbr_ref, bbi_ref):
    ar, ai = ar_ref[...], ai_ref[...]
    dt = jnp.exp(ldt_ref[...])
    er = jnp.exp(ar * dt)
    lbr = er * jnp.cos(ai * dt)
    lbi = er * jnp.sin(ai * dt)
    den = ar * ar + ai * ai
    nr = lbr - 1.0
    cr = (nr * ar + lbi * ai) / den
    ci = (lbi * ar - nr * ai) / den
    br, bi = br_ref[...], bi_ref[...]
    lr_ref[...] = lbr
    li_ref[...] = lbi
    bbr_ref[...] = cr * br - ci * bi
    bbi_ref[...] = cr * bi + ci * br


def _ssm_discretise(a_re, a_im, log_dt, b_re, b_im):
    g, p, n = b_re.shape
    rep = lambda a: jnp.repeat(a, n, axis=1)
    out = jax.ShapeDtypeStruct((g, p * n), F32)
    lr, li, bbr, bbi = pl.pallas_call(
        _ssm_disc_kernel, out_shape=(out, out, out, out), name="ssm_disc",
    )(rep(a_re), rep(a_im), jnp.broadcast_to(log_dt[:, None], (g, p * n)),
      b_re.reshape(g, p * n), b_im.reshape(g, p * n))
    lam_re = lr.reshape(g, p, n)[:, :, 0]
    lam_im = li.reshape(g, p, n)[:, :, 0]
    return lam_re, lam_im, bbr.reshape(g, p, n), bbi.reshape(g, p, n)


def _inproj_kernel(x_ref, sc_ref, sh_ref, w_ref, q_ref, k_ref, v_ref, u_ref, g_ref, *, q_scale):
    n = pl.program_id(0)
    bb, tt, d = x_ref.shape
    h = x_ref[...] * (1.0 + sc_ref[...]) + sh_ref[...]
    y = _dot(h.reshape(bb * tt, d).astype(BF16), w_ref[...])

    @pl.when(n == 0)
    def _():
        q_ref[...] = (y * q_scale).astype(BF16)

    @pl.when(n == 1)
    def _():
        k_ref[...] = y

    @pl.when(n == 2)
    def _():
        v_ref[...] = y

    @pl.when(n == 3)
    def _():
        u_ref[...] = y.reshape(u_ref.shape)

    @pl.when(n >= 4)
    def _():
        g_ref[...] = y


def _inproj(x, sc, sh, w_in16, *, bb, tt, u_time_major, head_dim):
    b, t, d = x.shape
    n_cols = w_in16.shape[1]
    aw = (n_cols - 2 * d) // 4
    tn = aw
    nb, nt = b // bb, t // tt
    m = bb * tt
    n_steps = n_cols // tn

    def park(n, lo, hi, idx, last):
        return jnp.where(n < lo, 0, jnp.where(n >= hi, last, idx))

    def tok_map(lo, hi, ncol):
        def f(n, bi, ti):
            row = park(n, lo, hi, bi * nt + ti, nb * nt - 1)
            return (row, jnp.clip(n - lo, 0, ncol - 1))
        return f

    if u_time_major:
        assert bb == 1
        u_shape = (t, b * aw)
        u_spec = pl.BlockSpec((tt, aw), lambda n, bi, ti: (park(n, 3, 4, ti, nt - 1), park(n, 3, 4, bi, nb - 1)))
    else:
        u_shape = (b * t, aw)
        u_spec = pl.BlockSpec((m, aw), tok_map(3, 4, 1))
    out_shape = (jax.ShapeDtypeStruct((b * t, aw), BF16),
                 jax.ShapeDtypeStruct((b * t, aw), F32),
                 jax.ShapeDtypeStruct((b * t, aw), F32),
                 jax.ShapeDtypeStruct(u_shape, F32),
                 jax.ShapeDtypeStruct((b * t, n_cols - 4 * aw), F32))
    return pl.pallas_call(
        functools.partial(_inproj_kernel, q_scale=head_dim ** -0.5),
        grid=(n_steps, nb, nt),
        in_specs=[pl.BlockSpec((bb, tt, d), lambda n, bi, ti: (bi, ti, 0)),
                  pl.BlockSpec((bb, 1, d), lambda n, bi, ti: (bi, 0, 0)),
                  pl.BlockSpec((bb, 1, d), lambda n, bi, ti: (bi, 0, 0)),
                  pl.BlockSpec((d, tn), lambda n, bi, ti: (0, n))],
        out_specs=(pl.BlockSpec((m, aw), tok_map(0, 1, 1)),
                   pl.BlockSpec((m, aw), tok_map(1, 2, 1)),
                   pl.BlockSpec((m, aw), tok_map(2, 3, 1)),
                   u_spec,
                   pl.BlockSpec((m, tn), tok_map(4, n_steps, n_steps - 4))),
        out_shape=out_shape,
        compiler_params=_cparams(("arbitrary", "arbitrary", "arbitrary")),
        name="inproj",
    )(x, sc, sh, w_in16)


def _attn_kernel(bias_ref, q_ref, k_ref, v_ref, o_ref, *, blk):
    h = pl.program_id(1)
    qi = pl.program_id(2)
    q = q_ref[...]
    bias = bias_ref[h]
    row = lax.broadcasted_iota(jnp.int32, (blk, blk), 0)
    col = lax.broadcasted_iota(jnp.int32, (blk, blk), 1)
    causal = col < row
    suffix = jnp.where(row > col, 1.0, 0.0).astype(BF16)

    def step(kb, carry, masked):
        acc, rsum = carry
        start = pl.multiple_of(kb * blk, blk)
        ks = k_ref[pl.ds(start, blk), :].astype(BF16)
        vs = v_ref[pl.ds(start, blk), :].astype(BF16)
        z = lax.dot_general(q, ks, (((1,), (1,)), ((), ())), preferred_element_type=F32) + bias
        sp = _softplus(z)
        if masked:
            sp = jnp.where(causal, sp, 0.0)
        later = _dot(sp.astype(BF16), suffix)
        w = jnp.exp(z - sp - later - rsum)
        if masked:
            w = jnp.where(causal, w, 0.0)
        acc = acc + _dot(w.astype(BF16), vs)
        rsum = rsum + jnp.sum(sp, axis=-1, keepdims=True)
        return acc, rsum

    carry = (jnp.zeros((blk, q.shape[1]), F32), jnp.zeros((blk, 1), F32))
    carry = step(qi, carry, True)
    acc, _ = lax.fori_loop(0, qi, lambda i, c: step(qi - 1 - i, c, False), carry)
    o_ref[...] = acc.astype(o_ref.dtype)


def _prompt_attention(q16, k, v, sb_bias, *, b, t, heads, head_dim):
    blk = min(256, t)
    nq = t // blk
    return pl.pallas_call(
        functools.partial(_attn_kernel, blk=blk),
        grid=(b, heads, nq),
        in_specs=[pl.BlockSpec(memory_space=pltpu.SMEM),
                  pl.BlockSpec((blk, head_dim), lambda bi, h, qi: (bi * nq + qi, h)),
                  pl.BlockSpec((t, head_dim), lambda bi, h, qi: (bi, h)),
                  pl.BlockSpec((t, head_dim), lambda bi, h, qi: (bi, h))],
        out_specs=pl.BlockSpec((blk, head_dim), lambda bi, h, qi: (bi * nq + qi, h)),
        out_shape=jax.ShapeDtypeStruct((b * t, heads * head_dim), BF16),
        compiler_params=_cparams(("arbitrary", "arbitrary", "arbitrary")),
        name="prompt_attn",
    )(sb_bias, q16, k, v)


def _sattn_kernel(pt_ref, qbd_ref, bias_ref, kn_ref, vn_ref, *refs, heads, pps, tq):
    kc_refs = refs[:pps]
    vc_refs = refs[pps:2 * pps]
    o_ref, acc_ref, r_ref, kpad_ref, vpad_ref = refs[2 * pps:]
    j = pl.program_id(1)
    page = kc_refs[0].shape[1]
    hd = o_ref.shape[2] // heads
    row = lax.broadcasted_iota(jnp.int32, (page, LANES), 0)
    col = lax.broadcasted_iota(jnp.int32, (page, LANES), 1)
    suffix_t = jnp.where(col > row, 1.0, 0.0).astype(BF16)

    def process(kblk, vblk, mask):
        z = _dot(kblk.astype(BF16), qbd_ref[0]) + bias_ref[...]
        sp = _softplus(z)
        if mask is not None:
            sp = jnp.where(mask, sp, 0.0)
        later = _dot(suffix_t, sp.astype(BF16))
        rsum = r_ref[...]
        w = jnp.exp(z - sp - later - rsum)
        if mask is not None:
            w = jnp.where(mask, w, 0.0)
        r_ref[...] = rsum + jnp.sum(sp, axis=0, keepdims=True)
        res = _dot(w.T.astype(BF16), vblk.astype(BF16))
        for h in range(heads):
            acc_ref[h * tq:(h + 1) * tq, :] += res[h * tq:(h + 1) * tq, h * hd:(h + 1) * hd]

    @pl.when(j == 0)
    def _():
        acc_ref[...] = jnp.zeros_like(acc_ref)
        r_ref[...] = jnp.zeros_like(r_ref)
        kpad_ref[...] = jnp.zeros_like(kpad_ref)
        vpad_ref[...] = jnp.zeros_like(vpad_ref)
        kpad_ref[0:tq, :] = kn_ref[0]
        vpad_ref[0:tq, :] = vn_ref[0]
        process(kpad_ref[...], vpad_ref[...], row < (col % tq))

    for i in range(pps):
        process(kc_refs[i][0], vc_refs[i][0], None)

    @pl.when(j == pl.num_programs(1) - 1)
    def _():
        for h in range(heads):
            o_ref[0, :, h * hd:(h + 1) * hd] = acc_ref[h * tq:(h + 1) * tq, :].astype(o_ref.dtype)


def _sample_attention(q16, k_new, v_new, cache_k, cache_v, page_table, sb_bias, *, heads, head_dim):
    b, n_pages = page_table.shape
    n_phys, page = cache_k.shape[0], cache_k.shape[1]
    aw = heads * head_dim
    tq = q16.shape[0] // b
    assert heads * tq <= LANES and page == LANES
    pps = 4
    assert n_pages % pps == 0
    q4 = q16.reshape(b, tq, heads, head_dim)
    qbd = jnp.einsum("bqhd,hg->bhdgq", q4, jnp.eye(heads, dtype=BF16)).reshape(b, aw, heads * tq)
    qbd = jnp.pad(qbd, ((0, 0), (0, 0), (0, LANES - heads * tq)))
    bias_row = jnp.pad(jnp.repeat(sb_bias, tq), (0, LANES - heads * tq)).reshape(1, LANES)
    kc = cache_k.reshape(n_phys, page, aw)
    vc = cache_v.reshape(n_phys, page, aw)

    def page_map(i):
        def f(bi, j, pt):
            return (pt[bi * n_pages + (n_pages - 1 - (j * pps + i))], 0, 0)
        return f

    page_specs = [pl.BlockSpec((1, page, aw), page_map(i)) for i in range(pps)]
    grid_spec = pltpu.PrefetchScalarGridSpec(
        num_scalar_prefetch=1,
        grid=(b, n_pages // pps),
        in_specs=[pl.BlockSpec((1, aw, LANES), lambda bi, j, pt: (bi, 0, 0)),
                  pl.BlockSpec((1, LANES), lambda bi, j, pt: (0, 0)),
                  pl.BlockSpec((1, tq, aw), lambda bi, j, pt: (bi, 0, 0)),
                  pl.BlockSpec((1, tq, aw), lambda bi, j, pt: (bi, 0, 0))] + page_specs + page_specs,
        out_specs=pl.BlockSpec((1, tq, aw), lambda bi, j, pt: (bi, 0, 0)),
        scratch_shapes=[pltpu.VMEM((LANES, head_dim), F32), pltpu.VMEM((1, LANES), F32),
                        pltpu.VMEM((page, aw), F32), pltpu.VMEM((page, aw), F32)])
    out = pl.pallas_call(
        functools.partial(_sattn_kernel, heads=heads, pps=pps, tq=tq),
        grid_spec=grid_spec,
        out_shape=jax.ShapeDtypeStruct((b, tq, aw), F32),
        compiler_params=_cparams(("arbitrary", "arbitrary")),
        name="sample_attn",
    )(page_table.reshape(-1), qbd, bias_row, k_new.reshape(b, tq, aw), v_new.reshape(b, tq, aw),
      *([kc] * pps), *([vc] * pps))
    return out.reshape(b * tq, aw).astype(BF16)


def _ssm_kernel(u_ref, h0r_ref, h0i_ref, br_ref, bi_ref, cr_ref, ci_ref, lr_ref, li_ref, d_ref,
                wg_ref, bg_ref, ys_ref, hr_out, hi_out,
                hs_r, hs_i, bu_r, bu_i, hh_r, hh_i, y_scr):
    tc = pl.program_id(1)
    tt, nb, w = u_ref.shape
    rows = tt * nb
    n_tiles = br_ref.shape[0]
    gw = br_ref.shape[1]
    sw = br_ref.shape[2]

    @pl.when(tc == 0)
    def _():
        hs_r[...] = h0r_ref[...]
        hs_i[...] = h0i_ref[...]

    u = u_ref[...].reshape(rows, w)
    u16 = u.astype(BF16)
    for j in range(n_tiles):
        uj = u16[:, j * gw:(j + 1) * gw]
        bu_r[...] = _dot(uj, br_ref[j])
        bu_i[...] = _dot(uj, bi_ref[j])
        lr = jnp.broadcast_to(lr_ref[:, j * sw:(j + 1) * sw], (nb, sw))
        li = jnp.broadcast_to(li_ref[:, j * sw:(j + 1) * sw], (nb, sw))

        def body(t, carry):
            hr, hi = carry
            r0 = pl.multiple_of(t * nb, nb)
            nr = lr * hr - li * hi + bu_r[pl.ds(r0, nb), :]
            ni = lr * hi + li * hr + bu_i[pl.ds(r0, nb), :]
            hh_r[pl.ds(r0, nb), :] = nr
            hh_i[pl.ds(r0, nb), :] = ni
            return nr, ni

        hr, hi = lax.fori_loop(0, tt, body, (hs_r[:, j * sw:(j + 1) * sw], hs_i[:, j * sw:(j + 1) * sw]),
                               unroll=8)
        hs_r[:, j * sw:(j + 1) * sw] = hr
        hs_i[:, j * sw:(j + 1) * sw] = hi
        yj = _dot(hh_r[...].astype(BF16), cr_ref[j]) + _dot(hh_i[...].astype(BF16), ci_ref[j])
        y_scr[:, j * gw:(j + 1) * gw] = yj + d_ref[:, j * gw:(j + 1) * gw] * u[:, j * gw:(j + 1) * gw]

    y = _gelu_tanh(y_scr[...])
    gate = jax.nn.sigmoid(_dot(y.astype(BF16), wg_ref[...]) + bg_ref[...])
    ys_ref[...] = (y * gate).reshape(ys_ref.shape)

    @pl.when(tc == pl.num_programs(1) - 1)
    def _():
        hr_out[...] = hs_r[...]
        hi_out[...] = hs_i[...]


def _ssm(u_tb, h0r, h0i, prm, *, tt):
    t, b, w = u_tb.shape
    nb = SUBLANES
    gp = h0r.shape[1]
    rows = tt * nb
    n_tiles, gw, sw = prm["b_re"].shape
    full = lambda a: pl.BlockSpec(a.shape, lambda bg, tc: (0,) * a.ndim)
    names = ["b_re", "b_im", "c_re", "c_imn", "lam_re", "lam_im", "d", "w_glu", "b_glu"]
    return pl.pallas_call(
        _ssm_kernel,
        grid=(b // nb, t // tt),
        in_specs=[pl.BlockSpec((tt, nb, w), lambda bg, tc: (tc, bg, 0)),
                  pl.BlockSpec((nb, gp), lambda bg, tc: (bg, 0)),
                  pl.BlockSpec((nb, gp), lambda bg, tc: (bg, 0))] + [full(prm[k]) for k in names],
        out_specs=(pl.BlockSpec((tt, nb, w), lambda bg, tc: (tc, bg, 0)),
                   pl.BlockSpec((nb, gp), lambda bg, tc: (bg, 0)),
                   pl.BlockSpec((nb, gp), lambda bg, tc: (bg, 0))),
        out_shape=(jax.ShapeDtypeStruct((t, b, w), F32),
                   jax.ShapeDtypeStruct((b, gp), F32),
                   jax.ShapeDtypeStruct((b, gp), F32)),
        scratch_shapes=[pltpu.VMEM((nb, gp), F32), pltpu.VMEM((nb, gp), F32),
                        pltpu.VMEM((rows, sw), F32), pltpu.VMEM((rows, sw), F32),
                        pltpu.VMEM((rows, sw), F32), pltpu.VMEM((rows, sw), F32),
                        pltpu.VMEM((rows, w), F32)],
        compiler_params=_cparams(("arbitrary", "arbitrary")),
        name="ssm",
    )(u_tb, h0r, h0i, *[prm[k] for k in names])


def _ssm_params(lam_re, lam_im, bb_re, bb_im, c_re, c_im, d_skip, w_glu, b_glu):
    g, p, n = bb_re.shape
    gpt = LANES // n
    nt = g // gpt
    eye = jnp.eye(gpt, dtype=F32)

    def bmat(a):
        return jnp.einsum("jgpn,gh->jgnhp", a.reshape(nt, gpt, p, n), eye).reshape(nt, gpt * n, gpt * p)

    def cmat(a):
        return jnp.einsum("jgnp,gh->jgphn", a.reshape(nt, gpt, n, p), eye).reshape(nt, gpt * p, gpt * n)

    return dict(b_re=bmat(bb_re).astype(BF16), b_im=bmat(bb_im).astype(BF16),
                c_re=cmat(c_re).astype(BF16), c_imn=cmat(-c_im).astype(BF16),
                lam_re=lam_re.reshape(1, g * p), lam_im=lam_im.reshape(1, g * p),
                d=d_skip.reshape(1, g * n), w_glu=w_glu.astype(BF16), b_glu=b_glu.reshape(1, -1))


def _merge_kernel(att_ref, ys_ref, g_ref, x_ref, g1_ref, sc2_ref, sh2_ref, wa_ref, ws_ref, wo_ref,
                  lng_ref, lnb_ref, x1_ref, h2_ref, *, alpha):
    bb, tt, d = x_ref.shape
    m = bb * tt
    a = _dot(att_ref[...], wa_ref[...])
    s = _dot(ys_ref[...].reshape(m, -1).astype(BF16), ws_ref[...])
    g = g_ref[...]
    merged = jax.nn.sigmoid(g[:, :d]) * a + jax.nn.sigmoid(g[:, d:]) * s
    o = _dot(merged.astype(BF16), wo_ref[...]).reshape(bb, tt, d)
    y = alpha * x_ref[...] + (1.0 + g1_ref[...]) * o
    x1 = _layernorm(y, lng_ref[...], lnb_ref[...])
    x1_ref[...] = x1
    h2_ref[...] = (x1 * (1.0 + sc2_ref[...]) + sh2_ref[...]).reshape(m, d).astype(BF16)


def _merge(att16, ys, ys_spec, gates, x, g1, sc2, sh2, wa16, ws16, wo16, ln_g, ln_b, *, bb, tt, alpha):
    b, t, d = x.shape
    nb, nt = b // bb, t // tt
    m = bb * tt
    aw = att16.shape[1]
    const = lambda a: pl.BlockSpec(a.shape, lambda bi, ti: (0,) * a.ndim, pipeline_mode=pl.Buffered(1))
    tok = lambda wd: pl.BlockSpec((m, wd), lambda bi, ti: (bi * nt + ti, 0))
    modspec = pl.BlockSpec((bb, 1, d), lambda bi, ti: (bi, 0, 0))
    return pl.pallas_call(
        functools.partial(_merge_kernel, alpha=alpha),
        grid=(nb, nt),
        in_specs=[tok(aw), ys_spec, tok(2 * d),
                  pl.BlockSpec((bb, tt, d), lambda bi, ti: (bi, ti, 0)),
                  modspec, modspec, modspec, const(wa16), const(ws16), const(wo16),
                  const(ln_g), const(ln_b)],
        out_specs=(pl.BlockSpec((bb, tt, d), lambda bi, ti: (bi, ti, 0)), tok(d)),
        out_shape=(jax.ShapeDtypeStruct((b, t, d), F32), jax.ShapeDtypeStruct((b * t, d), BF16)),
        compiler_params=_cparams(("arbitrary", "arbitrary")),
        name="merge_out",
    )(att16, ys, gates, x, g1, sc2, sh2, wa16, ws16, wo16, ln_g, ln_b)


def _ffn_kernel(h2_ref, x1_ref, g2_ref, wv_ref, wg_ref, cwv_ref, cwg_ref, cbv_ref, cbg_ref, wd_ref,
                lng_ref, lnb_ref, sv_ref, sg_ref, x2_ref, cov_ref, cog_ref, acc_ref, cv_ref, cg_ref,
                *, alpha):
    ti = pl.program_id(1)
    f = pl.program_id(2)
    bb, tt, d = x1_ref.shape
    m = bb * tt
    fc = wv_ref.shape[1]

    @pl.when(ti == 0)
    def _():
        cv_ref[f] = sv_ref[...]
        cg_ref[f] = sg_ref[...]

    @pl.when(f == 0)
    def _():
        acc_ref[...] = jnp.zeros_like(acc_ref)

    h2 = h2_ref[...]
    tidx = lax.broadcasted_iota(jnp.int32, (bb, tt, fc), 1)

    def conv_branch(w_ref, cw_ref, cb_ref, carry_ref, out_ref):
        up = _dot(h2, w_ref[...])
        st = carry_ref[f]
        s0, s1 = st[:, 0:1, :], st[:, 1:2, :]
        up3 = up.reshape(bb, tt, fc)
        p1 = pltpu.roll(up, 1, 0).reshape(bb, tt, fc)
        p2 = pltpu.roll(up, 2, 0).reshape(bb, tt, fc)
        p1 = jnp.where(tidx == 0, s1, p1)
        p2 = jnp.where(tidx == 0, s0, jnp.where(tidx == 1, s1, p2))
        cw = cw_ref[...]
        conv = cb_ref[...] + cw[0:1, :] * p2 + cw[1:2, :] * p1 + cw[2:3, :] * up3
        tail = up3[:, tt - 2:tt, :]
        carry_ref[f] = tail
        out_ref[...] = tail
        return conv.reshape(m, fc)

    val = conv_branch(wv_ref, cwv_ref, cbv_ref, cv_ref, cov_ref)
    gate = conv_branch(wg_ref, cwg_ref, cbg_ref, cg_ref, cog_ref)
    acc_ref[...] += _dot((_gelu_tanh(gate) * val).astype(BF16), wd_ref[...])

    @pl.when(f == pl.num_programs(2) - 1)
    def _():
        y = alpha * x1_ref[...] + (1.0 + g2_ref[...]) * acc_ref[...].reshape(bb, tt, d)
        x2_ref[...] = _layernorm(y, lng_ref[...], lnb_ref[...])


def _ffn(h2, x1, g2, w_up16, conv_w, conv_b, w_down16, ln_g, ln_b, state, *, bb, tt, alpha):
    b, t, d = x1.shape
    ff = w_down16.shape[0]
    fc = min(512, ff)
    nf = ff // fc
    nb, nt = b // bb, t // tt
    m = bb * tt
    const = lambda a: pl.BlockSpec(a.shape, lambda bi, ti, f: (0,) * a.ndim)
    modspec = pl.BlockSpec((bb, 1, d), lambda bi, ti, f: (bi, 0, 0))
    tail_spec = pl.BlockSpec((bb, 2, fc), lambda bi, ti, f: (bi, 0, jnp.where(ti == nt - 1, f, 0)))
    x2, cov, cog = pl.pallas_call(
        functools.partial(_ffn_kernel, alpha=alpha),
        grid=(nb, nt, nf),
        in_specs=[pl.BlockSpec((m, d), lambda bi, ti, f: (bi * nt + ti, 0)),
                  pl.BlockSpec((bb, tt, d), lambda bi, ti, f: (bi, ti, 0)),
                  modspec,
                  pl.BlockSpec((d, fc), lambda bi, ti, f: (0, f)),
                  pl.BlockSpec((d, fc), lambda bi, ti, f: (0, nf + f)),
                  pl.BlockSpec((conv_w.shape[0], fc), lambda bi, ti, f: (0, f)),
                  pl.BlockSpec((conv_w.shape[0], fc), lambda bi, ti, f: (0, nf + f)),
                  pl.BlockSpec((1, fc), lambda bi, ti, f: (0, f)),
                  pl.BlockSpec((1, fc), lambda bi, ti, f: (0, nf + f)),
                  pl.BlockSpec((fc, d), lambda bi, ti, f: (f, 0)),
                  const(ln_g), const(ln_b),
                  pl.BlockSpec((bb, 2, fc), lambda bi, ti, f: (bi, 0, f)),
                  pl.BlockSpec((bb, 2, fc), lambda bi, ti, f: (bi, 0, nf + f))],
        out_specs=(pl.BlockSpec((bb, tt, d), lambda bi, ti, f: (bi, ti, 0)),
                   tail_spec, tail_spec),
        out_shape=(jax.ShapeDtypeStruct((b, t, d), F32),
                   jax.ShapeDtypeStruct((b, 2, ff), F32),
                   jax.ShapeDtypeStruct((b, 2, ff), F32)),
        scratch_shapes=[pltpu.VMEM((m, d), F32),
                        pltpu.VMEM((nf, bb, 2, fc), F32), pltpu.VMEM((nf, bb, 2, fc), F32)],
        compiler_params=_cparams(("arbitrary", "arbitrary", "arbitrary")),
        name="conv_ffn",
    )(h2, x1, g2, w_up16, w_up16, conv_w, conv_w, conv_b, conv_b, w_down16, ln_g, ln_b, state, state)
    return x2, jnp.concatenate([cov, cog], axis=-1)


def _layer(x, mod, wts, ssm_prm, *, bb, tt, ssm_tt, prompt, attention, h0, conv_state, heads, head_dim, alpha):
    b, t, d = x.shape
    sh1, sc1, g1, sh2, sc2, g2 = mod
    q16, k, v, u, gates = _inproj(x, sc1, sh1, wts["w_in"], bb=bb, tt=tt, u_time_major=prompt,
                                  head_dim=head_dim)
    aw = k.shape[1]
    att16 = attention(q16, k, v)
    if prompt:
        u_tb = u.reshape(t, b, aw)
    else:
        u_tb = jnp.transpose(u.reshape(b, t, aw), (1, 0, 2))
    ys_tb, h_re, h_im = _ssm(u_tb, h0[0], h0[1], ssm_prm, tt=ssm_tt)
    tm = min(tt, 256) if prompt else tt
    nt = t // tm
    if prompt:
        ys = ys_tb.reshape(t, b * aw)
        ys_spec = pl.BlockSpec((tm, aw), lambda bi, ti: (ti, bi))
    else:
        ys = jnp.transpose(ys_tb, (1, 0, 2)).reshape(b * t, aw)
        ys_spec = pl.BlockSpec((bb * tm, aw), lambda bi, ti: (bi * nt + ti, 0))
    x1, h2 = _merge(att16, ys, ys_spec, gates, x, g1, sc2, sh2, wts["w_att_br"], wts["w_ssm_br"],
                    wts["w_out"], wts["ln1_g"], wts["ln1_b"], bb=bb, tt=tm, alpha=alpha)
    x2, conv_out = _ffn(h2, x1, g2, wts["w_up"], wts["conv_w"], wts["conv_b"], wts["w_down"],
                        wts["ln2_g"], wts["ln2_b"], conv_state, bb=bb, tt=tt, alpha=alpha)
    return x2, k, v, h_re, h_im, conv_out


def kernel(x_prompt, x_sample, cache_k, cache_v, state_ssm_re, state_ssm_im, state_conv, page_table, c_prompt, c_sample, w_ada, b_ada, w_in, w_att_br, w_ssm_br, w_out, sb_bias, ssm_a_re, ssm_a_im, ssm_log_dt, ssm_b_re, ssm_b_im, ssm_c_re, ssm_c_im, ssm_d, w_glu, b_glu, ln1_g, ln1_b, w_up, conv_w, conv_b, w_down, ln2_g, ln2_b):
    depth = w_in.shape[0]
    bp, tp, d = x_prompt.shape
    bs, ts, _ = x_sample.shape
    heads, head_dim = cache_k.shape[3], cache_k.shape[4]
    groups, state = ssm_a_re.shape[1], ssm_a_re.shape[2]
    alpha = (2.0 * depth) ** 0.25
    xp, xs = x_prompt, x_sample
    outs = [[] for _ in range(10)]
    c_all = jnp.concatenate([c_prompt, c_sample], axis=0)
    for l in range(depth):
        mod_all = _mod_table(c_all, w_ada[l], b_ada[l])
        mods = [a.reshape(-1, 1, d) for a in jnp.split(mod_all, N_MOD, axis=-1)]
        mod_p = [a[:bp] for a in mods]
        mod_s = [a[bp:] for a in mods]
        wts = dict(w_in=w_in[l].astype(BF16), w_att_br=w_att_br[l].astype(BF16),
                   w_ssm_br=w_ssm_br[l].astype(BF16), w_out=w_out[l].astype(BF16),
                   w_up=w_up[l].astype(BF16), w_down=w_down[l].astype(BF16),
                   conv_w=conv_w[l], conv_b=conv_b[l].reshape(1, -1),
                   ln1_g=ln1_g[l].reshape(1, d), ln1_b=ln1_b[l].reshape(1, d),
                   ln2_g=ln2_g[l].reshape(1, d), ln2_b=ln2_b[l].reshape(1, d))
        lam_re, lam_im, bb_re, bb_im = _ssm_discretise(ssm_a_re[l], ssm_a_im[l], ssm_log_dt[l],
                                                       ssm_b_re[l], ssm_b_im[l])
        ssm_prm = _ssm_params(lam_re, lam_im, bb_re, bb_im, ssm_c_re[l], ssm_c_im[l], ssm_d[l],
                              w_glu[l], b_glu[l])
        bias = sb_bias[l]

        zeros_h = jnp.zeros((bp, groups * state), F32)
        p_att = functools.partial(_prompt_attention, sb_bias=bias, b=bp, t=tp, heads=heads, head_dim=head_dim)
        xp, kp, vp, hrp, hip, cvp = _layer(
            xp, mod_p, wts, ssm_prm, bb=1, tt=min(512, tp), ssm_tt=min(64, tp), prompt=True, attention=p_att,
            h0=(zeros_h, zeros_h), conv_state=jnp.zeros((bp, 2, w_up.shape[2]), F32),
            heads=heads, head_dim=head_dim, alpha=alpha)

        s_att = functools.partial(_sample_attention, cache_k=cache_k[l], cache_v=cache_v[l],
                                  page_table=page_table, sb_bias=bias, heads=heads, head_dim=head_dim)
        xs, ks, vs, hrs, his, cvs = _layer(
            xs, mod_s, wts, ssm_prm, bb=bs, tt=ts, ssm_tt=ts, prompt=False, attention=s_att,
            h0=(state_ssm_re[l].reshape(bs, -1), state_ssm_im[l].reshape(bs, -1)),
            conv_state=state_conv[l], heads=heads, head_dim=head_dim, alpha=alpha)

        vals = (kp.reshape(bp, tp, heads, head_dim), vp.reshape(bp, tp, heads, head_dim),
                hrp.reshape(bp, groups, state), hip.reshape(bp, groups, state), cvp,
                ks.reshape(bs, ts, heads, head_dim), vs.reshape(bs, ts, heads, head_dim),
                hrs.reshape(bs, groups, state), his.reshape(bs, groups, state), cvs)
        for o, val in zip(outs, vals):
            o.append(val)
    return (xp, xs) + tuple(jnp.stack(o) for o in outs)
```

```python
import functools
import math

import jax
import jax.numpy as jnp
from jax import lax
from jax.experimental import pallas as pl
from jax.experimental.pallas import tpu as pltpu

F32 = jnp.float32
BF16 = jnp.bfloat16
LN_EPS = 1e-5
N_MOD = 6
V7X_VMEM_LIMIT_BYTES = 56 * 1024 * 1024
LANES = 128
SUBLANES = 8
LOG2E = math.log2(math.e)


def _cparams(sem):
    return pltpu.CompilerParams(dimension_semantics=sem, vmem_limit_bytes=V7X_VMEM_LIMIT_BYTES)


def _dot(a, b):
    return jnp.dot(a, b, preferred_element_type=F32)


def _gelu_tanh(x):
    c = math.sqrt(2.0 / math.pi)
    return 0.5 * x * (1.0 + jnp.tanh(c * (x + 0.044715 * (x * x * x))))


def _layernorm(x, g, b):
    mu = jnp.mean(x, axis=-1, keepdims=True)
    xc = x - mu
    var = jnp.mean(xc * xc, axis=-1, keepdims=True)
    return xc * lax.rsqrt(var + LN_EPS) * g + b


def _mod_kernel(c_ref, w_ref, b_ref, o_ref):
    c = c_ref[...]
    s = c * jax.nn.sigmoid(c)
    o_ref[...] = _dot(s.astype(BF16), w_ref[...].astype(BF16)) + b_ref[...]


def _mod_table(c, w_ada, b_ada):
    m, d = c.shape
    n = w_ada.shape[1]
    tn = min(1024, n)
    return pl.pallas_call(
        _mod_kernel,
        grid=(n // tn,),
        in_specs=[pl.BlockSpec((m, d), lambda j: (0, 0)),
                  pl.BlockSpec((d, tn), lambda j: (0, j)),
                  pl.BlockSpec((1, tn), lambda j: (0, j))],
        out_specs=pl.BlockSpec((m, tn), lambda j: (0, j)),
        out_shape=jax.ShapeDtypeStruct((m, n), F32),
        compiler_params=_cparams(("arbitrary",)),
        name="mod_table",
    )(c, w_ada, b_ada.reshape(1, n))


def _ssm_disc_kernel(ar_ref, ai_ref, ldt_ref, br_ref, bi_ref, lr_ref, li_ref, bbr_ref, bbi_ref):
    ar, ai = ar_ref[...], ai_ref[...]
    dt = jnp.exp(ldt_ref[...])
    er = jnp.exp(ar * dt)
    lbr = er * jnp.cos(ai * dt)
    lbi = er * jnp.sin(ai * dt)
    den = ar * ar + ai * ai
    nr = lbr - 1.0
    cr = (nr * ar + lbi * ai) / den
    ci = (lbi * ar - nr * ai) / den
    br, bi = br_ref[...], bi_ref[...]
    lr_ref[...] = lbr
    li_ref[...] = lbi
    bbr_ref[...] = cr * br - ci * bi
    bbi_ref[...] = cr * bi + ci * br


def _ssm_discretise(a_re, a_im, log_dt, b_re, b_im):
    g, p, n = b_re.shape
    rep = lambda a: jnp.repeat(a, n, axis=1)
    out = jax.ShapeDtypeStruct((g, p * n), F32)
    lr, li, bbr, bbi = pl.pallas_call(
        _ssm_disc_kernel, out_shape=(out, out, out, out), name="ssm_disc",
    )(rep(a_re), rep(a_im), jnp.broadcast_to(log_dt[:, None], (g, p * n)),
      b_re.reshape(g, p * n), b_im.reshape(g, p * n))
    lam_re = lr.reshape(g, p, n)[:, :, 0]
    lam_im = li.reshape(g, p, n)[:, :, 0]
    return lam_re, lam_im, bbr.reshape(g, p, n), bbi.reshape(g, p, n)


def _inproj_kernel(x_ref, sc_ref, sh_ref, w_ref, q_ref, k_ref, v_ref, u_ref, g_ref, *, q_scale):
    n = pl.program_id(0)
    bb, tt, d = x_ref.shape
    h = x_ref[...] * (1.0 + sc_ref[...]) + sh_ref[...]
    y = _dot(h.reshape(bb * tt, d).astype(BF16), w_ref[...])

    @pl.when(n == 0)
    def _():
        q_ref[...] = (y * q_scale).astype(BF16)

    @pl.when(n == 1)
    def _():
        k_ref[...] = y

    @pl.when(n == 2)
    def _():
        v_ref[...] = y

    @pl.when(n == 3)
    def _():
        u_ref[...] = y.reshape(u_ref.shape)

    @pl.when(n >= 4)
    def _():
        g_ref[...] = y


def _inproj(x, sc, sh, w_in16, *, bb, tt, u_time_major, head_dim):
    b, t, d = x.shape
    n_cols = w_in16.shape[1]
    aw = (n_cols - 2 * d) // 4
    tn = aw
    nb, nt = b // bb, t // tt
    m = bb * tt
    n_steps = n_cols // tn

    def park(n, lo, hi, idx, last):
        return jnp.where(n < lo, 0, jnp.where(n >= hi, last, idx))

    def tok_map(lo, hi, ncol):
        def f(n, bi, ti):
            row = park(n, lo, hi, bi * nt + ti, nb * nt - 1)
            return (row, jnp.clip(n - lo, 0, ncol - 1))
        return f

    if u_time_major:
        assert bb == 1
        u_shape = (t, b * aw)
        u_spec = pl.BlockSpec((tt, aw), lambda n, bi, ti: (park(n, 3, 4, ti, nt - 1), park(n, 3, 4, bi, nb - 1)))
    else:
        u_shape = (b * t, aw)
        u_spec = pl.BlockSpec((m, aw), tok_map(3, 4, 1))
    out_shape = (jax.ShapeDtypeStruct((b * t, aw), BF16),
                 jax.ShapeDtypeStruct((b * t, aw), F32),
                 jax.ShapeDtypeStruct((b * t, aw), F32),
                 jax.ShapeDtypeStruct(u_shape, F32),
                 jax.ShapeDtypeStruct((b * t, n_cols - 4 * aw), F32))
    return pl.pallas_call(
        functools.partial(_inproj_kernel, q_scale=LOG2E * head_dim ** -0.5),
        grid=(n_steps, nb, nt),
        in_specs=[pl.BlockSpec((bb, tt, d), lambda n, bi, ti: (bi, ti, 0)),
                  pl.BlockSpec((bb, 1, d), lambda n, bi, ti: (bi, 0, 0)),
                  pl.BlockSpec((bb, 1, d), lambda n, bi, ti: (bi, 0, 0)),
                  pl.BlockSpec((d, tn), lambda n, bi, ti: (0, n))],
        out_specs=(pl.BlockSpec((m, aw), tok_map(0, 1, 1)),
                   pl.BlockSpec((m, aw), tok_map(1, 2, 1)),
                   pl.BlockSpec((m, aw), tok_map(2, 3, 1)),
                   u_spec,
                   pl.BlockSpec((m, tn), tok_map(4, n_steps, n_steps - 4))),
        out_shape=out_shape,
        compiler_params=_cparams(("arbitrary", "arbitrary", "arbitrary")),
        name="inproj",
    )(x, sc, sh, w_in16)


def _sb_logits(z):
    m = jnp.maximum(z, 0.0)
    t = z - m
    l2 = jnp.log(1.0 + jnp.exp2(t - m)) * LOG2E
    return t, l2, m + l2


def _attn_kernel(bias_ref, q_ref, k_ref, v_ref, o_ref, *, blk, sub):
    h = pl.program_id(1)
    qi = pl.program_id(2)
    q = q_ref[...]
    bias = bias_ref[h]
    row = lax.broadcasted_iota(jnp.int32, (blk, sub), 0)
    col = lax.broadcasted_iota(jnp.int32, (blk, sub), 1)
    srow = lax.broadcasted_iota(jnp.int32, (sub, sub), 0)
    scol = lax.broadcasted_iota(jnp.int32, (sub, sub), 1)
    suffix = jnp.where(srow > scol, 1.0, 0.0).astype(BF16)

    def step(kb, carry, masked):
        acc, rsum = carry
        start = pl.multiple_of(kb * blk, blk)
        subs = list(reversed(range(blk // sub)))
        zs, vss = [], []
        for c in subs:
            ks = k_ref[pl.ds(start + c * sub, sub), :].astype(BF16)
            vss.append(v_ref[pl.ds(start + c * sub, sub), :].astype(BF16))
            zs.append(lax.dot_general(q, ks, (((1,), (1,)), ((), ())), preferred_element_type=F32) + bias)
        logits = [_sb_logits(z) for z in zs]
        masks = [(col + c * sub) < row for c in subs] if masked else None
        sps = [jnp.where(masks[i], lg[2], 0.0) if masked else lg[2] for i, lg in enumerate(logits)]
        laters = [_dot(sp.astype(BF16), suffix) for sp in sps]
        ws = [jnp.exp2(lg[0] - lg[1] - later) for lg, later in zip(logits, laters)]
        if masked:
            ws = [jnp.where(mk, w, 0.0) for mk, w in zip(masks, ws)]
        parts = [_dot(w.astype(BF16), vs) for w, vs in zip(ws, vss)]
        sums = [jnp.sum(sp, axis=-1, keepdims=True) for sp in sps]
        pv, tail = parts[0], sums[0]
        for part, rs in zip(parts[1:], sums[1:]):
            pv = pv + jnp.exp2(-tail) * part
            tail = tail + rs
        return acc + jnp.exp2(-rsum) * pv, rsum + tail

    carry = (jnp.zeros((blk, q.shape[1]), F32), jnp.zeros((blk, 1), F32))
    carry = step(qi, carry, True)
    acc, _ = lax.fori_loop(0, qi, lambda i, c: step(qi - 1 - i, c, False), carry)
    o_ref[...] = acc.astype(o_ref.dtype)


def _prompt_attention(q16, k, v, bias2, *, b, t, heads, head_dim):
    blk = min(512, t)
    sub = min(256, blk)
    nq = t // blk
    return pl.pallas_call(
        functools.partial(_attn_kernel, blk=blk, sub=sub),
        grid=(b, heads, nq),
        in_specs=[pl.BlockSpec(memory_space=pltpu.SMEM),
                  pl.BlockSpec((blk, head_dim), lambda bi, h, qi: (bi * nq + qi, h)),
                  pl.BlockSpec((t, head_dim), lambda bi, h, qi: (bi, h)),
                  pl.BlockSpec((t, head_dim), lambda bi, h, qi: (bi, h))],
        out_specs=pl.BlockSpec((blk, head_dim), lambda bi, h, qi: (bi * nq + qi, h)),
        out_shape=jax.ShapeDtypeStruct((b * t, heads * head_dim), BF16),
        compiler_params=_cparams(("arbitrary", "arbitrary", "arbitrary")),
        name="prompt_attn",
    )(bias2, q16, k, v)


def _sattn_kernel(pt_ref, qbd_ref, bias_ref, kn_ref, vn_ref, *refs, heads, pps, tq):
    kc_refs = refs[:pps]
    vc_refs = refs[pps:2 * pps]
    o_ref, acc_ref, r_ref, kpad_ref, vpad_ref = refs[2 * pps:]
    j = pl.program_id(1)
    page, hd = kc_refs[0].shape[1] // heads, kc_refs[0].shape[2]
    row = lax.broadcasted_iota(jnp.int32, (page, LANES), 0)
    col = lax.broadcasted_iota(jnp.int32, (page, LANES), 1)
    suffix_t = jnp.where(col > row, 1.0, 0.0).astype(BF16)

    def process(blocks, mask):
        zs = [_dot(kblk.astype(BF16), qbd_ref[0]) + bias_ref[...] for kblk, _ in blocks]
        logits = [_sb_logits(z) for z in zs]
        sps = [lg[2] if mask is None else jnp.where(mask, lg[2], 0.0) for lg in logits]
        laters = [_dot(suffix_t, sp.astype(BF16)) for sp in sps]
        rsum = r_ref[...]
        ws = []
        for lg, sp, later in zip(logits, sps, laters):
            w = jnp.exp2(lg[0] - lg[1] - later - rsum)
            ws.append(w if mask is None else jnp.where(mask, w, 0.0))
            rsum = rsum + jnp.sum(sp, axis=0, keepdims=True)
        r_ref[...] = rsum
        wts = [w.T.astype(BF16) for w in ws]
        res = None
        for wt, (_, vblk) in zip(wts, blocks):
            part = _dot(wt, vblk.astype(BF16))
            res = part if res is None else res + part
        for h in range(heads):
            acc_ref[h * tq:(h + 1) * tq, :] += res[h * tq:(h + 1) * tq, h * hd:(h + 1) * hd]

    @pl.when(j == 0)
    def _():
        acc_ref[...] = jnp.zeros_like(acc_ref)
        r_ref[...] = jnp.zeros_like(r_ref)
        kpad_ref[...] = jnp.zeros_like(kpad_ref)
        vpad_ref[...] = jnp.zeros_like(vpad_ref)
        kpad_ref[0:tq, :] = kn_ref[0]
        vpad_ref[0:tq, :] = vn_ref[0]
        process([(kpad_ref[...], vpad_ref[...])], row < (col % tq))

    def page_rows(ref):
        return jnp.concatenate([ref[0, pl.ds(h, page, stride=heads), :] for h in range(heads)], axis=-1)

    process([(page_rows(kc_refs[i]), page_rows(vc_refs[i])) for i in range(pps)], None)

    @pl.when(j == pl.num_programs(1) - 1)
    def _():
        for h in range(heads):
            o_ref[0, :, h * hd:(h + 1) * hd] = acc_ref[h * tq:(h + 1) * tq, :].astype(o_ref.dtype)


def _sample_attention(q16, k_new, v_new, cache_k, cache_v, layer, page_table, bias2, *, heads, head_dim):
    b, n_pages = page_table.shape
    depth, n_phys, page = cache_k.shape[0], cache_k.shape[1], cache_k.shape[2]
    aw = heads * head_dim
    tq = q16.shape[0] // b
    assert heads * tq <= LANES and page == LANES
    pps = 8
    assert n_pages % pps == 0
    q4 = q16.reshape(b, tq, heads, head_dim)
    qbd = jnp.einsum("bqhd,hg->bhdgq", q4, jnp.eye(heads, dtype=BF16)).reshape(b, aw, heads * tq)
    qbd = jnp.pad(qbd, ((0, 0), (0, 0), (0, LANES - heads * tq)))
    bias_row = jnp.pad(jnp.repeat(bias2, tq), (0, LANES - heads * tq)).reshape(1, LANES)
    kc = cache_k.reshape(depth * n_phys, page * heads, head_dim)
    vc = cache_v.reshape(depth * n_phys, page * heads, head_dim)
    base = layer * n_phys

    def page_map(i):
        def f(bi, j, pt):
            return (base + pt[bi * n_pages + (n_pages - 1 - (j * pps + i))], 0, 0)
        return f

    page_specs = [pl.BlockSpec((1, page * heads, head_dim), page_map(i)) for i in range(pps)]
    grid_spec = pltpu.PrefetchScalarGridSpec(
        num_scalar_prefetch=1,
        grid=(b, n_pages // pps),
        in_specs=[pl.BlockSpec((1, aw, LANES), lambda bi, j, pt: (bi, 0, 0)),
                  pl.BlockSpec((1, LANES), lambda bi, j, pt: (0, 0)),
                  pl.BlockSpec((1, tq, aw), lambda bi, j, pt: (bi, 0, 0)),
                  pl.BlockSpec((1, tq, aw), lambda bi, j, pt: (bi, 0, 0))] + page_specs + page_specs,
        out_specs=pl.BlockSpec((1, tq, aw), lambda bi, j, pt: (bi, 0, 0)),
        scratch_shapes=[pltpu.VMEM((LANES, head_dim), F32), pltpu.VMEM((1, LANES), F32),
                        pltpu.VMEM((page, aw), F32), pltpu.VMEM((page, aw), F32)])
    out = pl.pallas_call(
        functools.partial(_sattn_kernel, heads=heads, pps=pps, tq=tq),
        grid_spec=grid_spec,
        out_shape=jax.ShapeDtypeStruct((b, tq, aw), F32),
        compiler_params=_cparams(("arbitrary", "arbitrary")),
        name="sample_attn",
    )(page_table.reshape(-1), qbd, bias_row, k_new.reshape(b, tq, aw), v_new.reshape(b, tq, aw),
      *([kc] * pps), *([vc] * pps))
    return out.reshape(b * tq, aw).astype(BF16)


def _ssm_kernel(u_ref, h0r_ref, h0i_ref, br_ref, bi_ref, cr_ref, ci_ref, lr_ref, li_ref, d_ref,
                wg_ref, bg_ref, ys_ref, hr_out, hi_out,
                hs_r, hs_i, bu_r, bu_i, hh_r, hh_i, y_scr):
    tc = pl.program_id(1)
    tt, nb, w = u_ref.shape
    rows = tt * nb
    n_tiles = br_ref.shape[0]
    gw = br_ref.shape[1]
    sw = br_ref.shape[2]

    @pl.when(tc == 0)
    def _():
        hs_r[...] = h0r_ref[...]
        hs_i[...] = h0i_ref[...]

    u = u_ref[...].reshape(rows, w)
    u16 = u.astype(BF16)
    for j in range(n_tiles):
        uj = u16[:, j * gw:(j + 1) * gw]
        bu_r[...] = _dot(uj, br_ref[j])
        bu_i[...] = _dot(uj, bi_ref[j])
        lr = jnp.broadcast_to(lr_ref[:, j * sw:(j + 1) * sw], (nb, sw))
        li = jnp.broadcast_to(li_ref[:, j * sw:(j + 1) * sw], (nb, sw))

        def body(t, carry):
            hr, hi = carry
            r0 = pl.multiple_of(t * nb, nb)
            nr = lr * hr - li * hi + bu_r[pl.ds(r0, nb), :]
            ni = lr * hi + li * hr + bu_i[pl.ds(r0, nb), :]
            hh_r[pl.ds(r0, nb), :] = nr
            hh_i[pl.ds(r0, nb), :] = ni
            return nr, ni

        hr, hi = lax.fori_loop(0, tt, body, (hs_r[:, j * sw:(j + 1) * sw], hs_i[:, j * sw:(j + 1) * sw]),
                               unroll=8)
        hs_r[:, j * sw:(j + 1) * sw] = hr
        hs_i[:, j * sw:(j + 1) * sw] = hi
        yj = _dot(hh_r[...].astype(BF16), cr_ref[j]) + _dot(hh_i[...].astype(BF16), ci_ref[j])
        y_scr[:, j * gw:(j + 1) * gw] = yj + d_ref[:, j * gw:(j + 1) * gw] * u[:, j * gw:(j + 1) * gw]

    y = _gelu_tanh(y_scr[...])
    gate = jax.nn.sigmoid(_dot(y.astype(BF16), wg_ref[...]) + bg_ref[...])
    ys_ref[...] = (y * gate).reshape(ys_ref.shape)

    @pl.when(tc == pl.num_programs(1) - 1)
    def _():
        hr_out[...] = hs_r[...]
        hi_out[...] = hs_i[...]


def _ssm(u_tb, h0r, h0i, prm, *, tt):
    t, b, w = u_tb.shape
    nb = SUBLANES
    gp = h0r.shape[1]
    rows = tt * nb
    n_tiles, gw, sw = prm["b_re"].shape
    full = lambda a: pl.BlockSpec(a.shape, lambda bg, tc: (0,) * a.ndim)
    names = ["b_re", "b_im", "c_re", "c_imn", "lam_re", "lam_im", "d", "w_glu", "b_glu"]
    return pl.pallas_call(
        _ssm_kernel,
        grid=(b // nb, t // tt),
        in_specs=[pl.BlockSpec((tt, nb, w), lambda bg, tc: (tc, bg, 0)),
                  pl.BlockSpec((nb, gp), lambda bg, tc: (bg, 0)),
                  pl.BlockSpec((nb, gp), lambda bg, tc: (bg, 0))] + [full(prm[k]) for k in names],
        out_specs=(pl.BlockSpec((tt, nb, w), lambda bg, tc: (tc, bg, 0)),
                   pl.BlockSpec((nb, gp), lambda bg, tc: (bg, 0)),
                   pl.BlockSpec((nb, gp), lambda bg, tc: (bg, 0))),
        out_shape=(jax.ShapeDtypeStruct((t, b, w), F32),
                   jax.ShapeDtypeStruct((b, gp), F32),
                   jax.ShapeDtypeStruct((b, gp), F32)),
        scratch_shapes=[pltpu.VMEM((nb, gp), F32), pltpu.VMEM((nb, gp), F32),
                        pltpu.VMEM((rows, sw), F32), pltpu.VMEM((rows, sw), F32),
                        pltpu.VMEM((rows, sw), F32), pltpu.VMEM((rows, sw), F32),
                        pltpu.VMEM((rows, w), F32)],
        compiler_params=_cparams(("arbitrary", "arbitrary")),
        name="ssm",
    )(u_tb, h0r, h0i, *[prm[k] for k in names])


def _ssm_params(lam_re, lam_im, bb_re, bb_im, c_re, c_im, d_skip, w_glu, b_glu):
    g, p, n = bb_re.shape
    gpt = LANES // n
    nt = g // gpt
    eye = jnp.eye(gpt, dtype=F32)

    def bmat(a):
        return jnp.einsum("jgpn,gh->jgnhp", a.reshape(nt, gpt, p, n), eye).reshape(nt, gpt * n, gpt * p)

    def cmat(a):
        return jnp.einsum("jgnp,gh->jgphn", a.reshape(nt, gpt, n, p), eye).reshape(nt, gpt * p, gpt * n)

    return dict(b_re=bmat(bb_re).astype(BF16), b_im=bmat(bb_im).astype(BF16),
                c_re=cmat(c_re).astype(BF16), c_imn=cmat(-c_im).astype(BF16),
                lam_re=lam_re.reshape(1, g * p), lam_im=lam_im.reshape(1, g * p),
                d=d_skip.reshape(1, g * n), w_glu=w_glu.astype(BF16), b_glu=b_glu.reshape(1, -1))


def _merge_kernel(att_ref, ys_ref, g_ref, x_ref, g1_ref, sc2_ref, sh2_ref, wa_ref, ws_ref, wo_ref,
                  lng_ref, lnb_ref, x1_ref, h2_ref, *, alpha):
    bb, tt, d = x_ref.shape
    m = bb * tt
    a = _dot(att_ref[...], wa_ref[...])
    s = _dot(ys_ref[...].reshape(m, -1).astype(BF16), ws_ref[...])
    g = g_ref[...]
    merged = jax.nn.sigmoid(g[:, :d]) * a + jax.nn.sigmoid(g[:, d:]) * s
    o = _dot(merged.astype(BF16), wo_ref[...]).reshape(bb, tt, d)
    y = alpha * x_ref[...] + (1.0 + g1_ref[...]) * o
    x1 = _layernorm(y, lng_ref[...], lnb_ref[...])
    x1_ref[...] = x1
    h2_ref[...] = (x1 * (1.0 + sc2_ref[...]) + sh2_ref[...]).reshape(m, d).astype(BF16)


def _merge(att16, ys, ys_spec, gates, x, g1, sc2, sh2, wa16, ws16, wo16, ln_g, ln_b, *, bb, tt, alpha):
    b, t, d = x.shape
    nb, nt = b // bb, t // tt
    m = bb * tt
    aw = att16.shape[1]
    const = lambda a: pl.BlockSpec(a.shape, lambda bi, ti: (0,) * a.ndim, pipeline_mode=pl.Buffered(1))
    tok = lambda wd: pl.BlockSpec((m, wd), lambda bi, ti: (bi * nt + ti, 0))
    modspec = pl.BlockSpec((bb, 1, d), lambda bi, ti: (bi, 0, 0))
    return pl.pallas_call(
        functools.partial(_merge_kernel, alpha=alpha),
        grid=(nb, nt),
        in_specs=[tok(aw), ys_spec, tok(2 * d),
                  pl.BlockSpec((bb, tt, d), lambda bi, ti: (bi, ti, 0)),
                  modspec, modspec, modspec, const(wa16), const(ws16), const(wo16),
                  const(ln_g), const(ln_b)],
        out_specs=(pl.BlockSpec((bb, tt, d), lambda bi, ti: (bi, ti, 0)), tok(d)),
        out_shape=(jax.ShapeDtypeStruct((b, t, d), F32), jax.ShapeDtypeStruct((b * t, d), BF16)),
        compiler_params=_cparams(("arbitrary", "arbitrary")),
        name="merge_out",
    )(att16, ys, gates, x, g1, sc2, sh2, wa16, ws16, wo16, ln_g, ln_b)


def _ffn_kernel(h2_ref, x1_ref, g2_ref, wv_ref, wg_ref, cwv_ref, cwg_ref, cbv_ref, cbg_ref, wd_ref,
                lng_ref, lnb_ref, sv_ref, sg_ref, x2_ref, cov_ref, cog_ref,
                acc_ref, cv_ref, cg_ref, upv_ref, upg_ref, *, alpha):
    ti = pl.program_id(1)
    f = pl.program_id(2)
    nf = pl.num_programs(2) - 1
    fb = jnp.maximum(f - 1, 0)
    bb, tt, d = x1_ref.shape
    m = bb * tt
    fc = wv_ref.shape[1]

    @pl.when(f == 0)
    def _():
        acc_ref[...] = jnp.zeros_like(acc_ref)
        upv_ref[...] = jnp.zeros_like(upv_ref)
        upg_ref[...] = jnp.zeros_like(upg_ref)

    @pl.when(ti == 0)
    def _():
        cv_ref[fb] = sv_ref[...]
        cg_ref[fb] = sg_ref[...]

    tidx = lax.broadcasted_iota(jnp.int32, (bb, tt, fc), 1)

    def conv_branch(up_ref, cw_ref, cb_ref, carry_ref, out_ref):
        up = up_ref[...]
        st = carry_ref[fb]
        s0, s1 = st[:, 0:1, :], st[:, 1:2, :]
        up3 = up.reshape(bb, tt, fc)
        p1 = pltpu.roll(up, 1, 0).reshape(bb, tt, fc)
        p2 = pltpu.roll(up, 2, 0).reshape(bb, tt, fc)
        p1 = jnp.where(tidx == 0, s1, p1)
        p2 = jnp.where(tidx == 0, s0, jnp.where(tidx == 1, s1, p2))
        cw = cw_ref[...]
        conv = cb_ref[...] + cw[0:1, :] * p2 + cw[1:2, :] * p1 + cw[2:3, :] * up3
        tail = up3[:, tt - 2:tt, :]
        carry_ref[fb] = jnp.where(f > 0, tail, st)
        out_ref[...] = tail
        return conv.reshape(m, fc)

    h2 = h2_ref[...]
    new_v = _dot(h2, wv_ref[...])
    new_g = _dot(h2, wg_ref[...])
    val = conv_branch(upv_ref, cwv_ref, cbv_ref, cv_ref, cov_ref)
    gate = conv_branch(upg_ref, cwg_ref, cbg_ref, cg_ref, cog_ref)
    prod = (_gelu_tanh(gate) * val).astype(BF16)
    acc_ref[...] += _dot(prod, wd_ref[...])
    upv_ref[...] = new_v
    upg_ref[...] = new_g

    @pl.when(f == 0)
    def _():
        acc_ref[...] = jnp.zeros_like(acc_ref)

    @pl.when(f == nf)
    def _():
        y = alpha * x1_ref[...] + (1.0 + g2_ref[...]) * acc_ref[...].reshape(bb, tt, d)
        x2_ref[...] = _layernorm(y, lng_ref[...], lnb_ref[...])


def _ffn(h2, x1, g2, w_up16, conv_w, conv_b, w_down16, ln_g, ln_b, state, *, bb, tt, alpha):
    b, t, d = x1.shape
    ff = w_down16.shape[0]
    fc = min(512, ff)
    nf = ff // fc
    nb, nt = b // bb, t // tt
    m = bb * tt
    kw = conv_w.shape[0]
    const = lambda a: pl.BlockSpec(a.shape, lambda bi, ti, f: (0,) * a.ndim)
    modspec = pl.BlockSpec((bb, 1, d), lambda bi, ti, f: (bi, 0, 0))
    up_i = lambda f: jnp.minimum(f, nf - 1)
    fin_i = lambda f: jnp.maximum(f - 1, 0)
    tail_spec = pl.BlockSpec((bb, 2, fc), lambda bi, ti, f: (bi, 0, jnp.where(ti == nt - 1, fin_i(f), 0)))
    x2, cov, cog = pl.pallas_call(
        functools.partial(_ffn_kernel, alpha=alpha),
        grid=(nb, nt, nf + 1),
        in_specs=[pl.BlockSpec((m, d), lambda bi, ti, f: (bi * nt + ti, 0)),
                  pl.BlockSpec((bb, tt, d), lambda bi, ti, f: (bi, ti, 0)),
                  modspec,
                  pl.BlockSpec((d, fc), lambda bi, ti, f: (0, up_i(f))),
                  pl.BlockSpec((d, fc), lambda bi, ti, f: (0, nf + up_i(f))),
                  pl.BlockSpec((kw, fc), lambda bi, ti, f: (0, fin_i(f))),
                  pl.BlockSpec((kw, fc), lambda bi, ti, f: (0, nf + fin_i(f))),
                  pl.BlockSpec((1, fc), lambda bi, ti, f: (0, fin_i(f))),
                  pl.BlockSpec((1, fc), lambda bi, ti, f: (0, nf + fin_i(f))),
                  pl.BlockSpec((fc, d), lambda bi, ti, f: (fin_i(f), 0)),
                  const(ln_g), const(ln_b),
                  pl.BlockSpec((bb, 2, fc), lambda bi, ti, f: (bi, 0, fin_i(f))),
                  pl.BlockSpec((bb, 2, fc), lambda bi, ti, f: (bi, 0, nf + fin_i(f)))],
        out_specs=(pl.BlockSpec((bb, tt, d), lambda bi, ti, f: (bi, ti, 0)),
                   tail_spec, tail_spec),
        out_shape=(jax.ShapeDtypeStruct((b, t, d), F32),
                   jax.ShapeDtypeStruct((b, 2, ff), F32),
                   jax.ShapeDtypeStruct((b, 2, ff), F32)),
        scratch_shapes=[pltpu.VMEM((m, d), F32),
                        pltpu.VMEM((nf, bb, 2, fc), F32), pltpu.VMEM((nf, bb, 2, fc), F32),
                        pltpu.VMEM((m, fc), F32), pltpu.VMEM((m, fc), F32)],
        compiler_params=_cparams(("arbitrary", "arbitrary", "arbitrary")),
        name="conv_ffn",
    )(h2, x1, g2, w_up16, w_up16, conv_w, conv_w, conv_b, conv_b, w_down16, ln_g, ln_b, state, state)
    return x2, jnp.concatenate([cov, cog], axis=-1)


def _layer(x, mod, wts, ssm_prm, *, bb, tt, ssm_tt, prompt, attention, h0, conv_state, heads, head_dim, alpha):
    b, t, d = x.shape
    sh1, sc1, g1, sh2, sc2, g2 = mod
    q16, k, v, u, gates = _inproj(x, sc1, sh1, wts["w_in"], bb=bb, tt=tt, u_time_major=prompt,
                                  head_dim=head_dim)
    aw = k.shape[1]
    att16 = attention(q16, k, v)
    if prompt:
        u_tb = u.reshape(t, b, aw)
    else:
        u_tb = jnp.transpose(u.reshape(b, t, aw), (1, 0, 2))
    ys_tb, h_re, h_im = _ssm(u_tb, h0[0], h0[1], ssm_prm, tt=ssm_tt)
    tm = min(tt, 256) if prompt else tt
    nt = t // tm
    if prompt:
        ys = ys_tb.reshape(t, b * aw)
        ys_spec = pl.BlockSpec((tm, aw), lambda bi, ti: (ti, bi))
    else:
        ys = jnp.transpose(ys_tb, (1, 0, 2)).reshape(b * t, aw)
        ys_spec = pl.BlockSpec((bb * tm, aw), lambda bi, ti: (bi * nt + ti, 0))
    x1, h2 = _merge(att16, ys, ys_spec, gates, x, g1, sc2, sh2, wts["w_att_br"], wts["w_ssm_br"],
                    wts["w_out"], wts["ln1_g"], wts["ln1_b"], bb=bb, tt=tm, alpha=alpha)
    x2, conv_out = _ffn(h2, x1, g2, wts["w_up"], wts["conv_w"], wts["conv_b"], wts["w_down"],
                        wts["ln2_g"], wts["ln2_b"], conv_state, bb=bb, tt=tt, alpha=alpha)
    return x2, k, v, h_re, h_im, conv_out


def kernel(x_prompt, x_sample, cache_k, cache_v, state_ssm_re, state_ssm_im, state_conv, page_table, c_prompt, c_sample, w_ada, b_ada, w_in, w_att_br, w_ssm_br, w_out, sb_bias, ssm_a_re, ssm_a_im, ssm_log_dt, ssm_b_re, ssm_b_im, ssm_c_re, ssm_c_im, ssm_d, w_glu, b_glu, ln1_g, ln1_b, w_up, conv_w, conv_b, w_down, ln2_g, ln2_b):
    depth = w_in.shape[0]
    bp, tp, d = x_prompt.shape
    bs, ts, _ = x_sample.shape
    heads, head_dim = cache_k.shape[3], cache_k.shape[4]
    groups, state = ssm_a_re.shape[1], ssm_a_re.shape[2]
    alpha = (2.0 * depth) ** 0.25
    xp, xs = x_prompt, x_sample
    outs = [[] for _ in range(10)]
    c_all = jnp.concatenate([c_prompt, c_sample], axis=0)
    for l in range(depth):
        mod_all = _mod_table(c_all, w_ada[l], b_ada[l])
        mods = [a.reshape(-1, 1, d) for a in jnp.split(mod_all, N_MOD, axis=-1)]
        mod_p = [a[:bp] for a in mods]
        mod_s = [a[bp:] for a in mods]
        wts = dict(w_in=w_in[l].astype(BF16), w_att_br=w_att_br[l].astype(BF16),
                   w_ssm_br=w_ssm_br[l].astype(BF16), w_out=w_out[l].astype(BF16),
                   w_up=w_up[l].astype(BF16), w_down=w_down[l].astype(BF16),
                   conv_w=conv_w[l], conv_b=conv_b[l].reshape(1, -1),
                   ln1_g=ln1_g[l].reshape(1, d), ln1_b=ln1_b[l].reshape(1, d),
                   ln2_g=ln2_g[l].reshape(1, d), ln2_b=ln2_b[l].reshape(1, d))
        lam_re, lam_im, bb_re, bb_im = _ssm_discretise(ssm_a_re[l], ssm_a_im[l], ssm_log_dt[l],
                                                       ssm_b_re[l], ssm_b_im[l])
        ssm_prm = _ssm_params(lam_re, lam_im, bb_re, bb_im, ssm_c_re[l], ssm_c_im[l], ssm_d[l],
                              w_glu[l], b_glu[l])
        bias2 = sb_bias[l] * LOG2E

        zeros_h = jnp.zeros((bp, groups * state), F32)
        p_att = functools.partial(_prompt_attention, bias2=bias2, b=bp, t=tp, heads=heads, head_dim=head_dim)
        xp, kp, vp, hrp, hip, cvp = _layer(
            xp, mod_p, wts, ssm_prm, bb=1, tt=min(512, tp), ssm_tt=min(64, tp), prompt=True, attention=p_att,
            h0=(zeros_h, zeros_h), conv_state=jnp.zeros((bp, 2, w_up.shape[2]), F32),
            heads=heads, head_dim=head_dim, alpha=alpha)

        s_att = functools.partial(_sample_attention, cache_k=cache_k, cache_v=cache_v, layer=l,
                                  page_table=page_table, bias2=bias2, heads=heads, head_dim=head_dim)
        xs, ks, vs, hrs, his, cvs = _layer(
            xs, mod_s, wts, ssm_prm, bb=bs, tt=ts, ssm_tt=ts, prompt=False, attention=s_att,
            h0=(state_ssm_re[l].reshape(bs, -1), state_ssm_im[l].reshape(bs, -1)),
            conv_state=state_conv[l], heads=heads, head_dim=head_dim, alpha=alpha)

        vals = (kp.reshape(bp, tp, heads, head_dim), vp.reshape(bp, tp, heads, head_dim),
                hrp.reshape(bp, groups, state), hip.reshape(bp, groups, state), cvp,
                ks.reshape(bs, ts, heads, head_dim), vs.reshape(bs, ts, heads, head_dim),
                hrs.reshape(bs, groups, state), his.reshape(bs, groups, state), cvs)
        for o, val in zip(outs, vals):
            o.append(val)
    return (xp, xs) + tuple(jnp.stack(o) for o in outs)
```
